```python
import math
import jax, jax.numpy as jnp
from jax import lax
import numpy as np

D_MODEL = 1024
BATCH = 4
SEQ = 4096
DEPTH = 4

HEAD_DIM = 64
N_ATT_HEADS = 8
ATT_WIDTH = N_ATT_HEADS * HEAD_DIM
N_CONV_GROUPS = 8
CONV_WIDTH = 512
MIX_WIDTH = ATT_WIDTH + CONV_WIDTH
IN_COLS = 4 * ATT_WIDTH + 4 * CONV_WIDTH
CONV_K = 3
DILATED_CONFIGS = ((128, 1), (512, 4), (2048, 16))
NUM_BUCKETS = 32
MAX_DISTANCE = 1024
Q_BLOCK = 128
EPS = 1e-6

kernel_name = "hymba_dilated_attn_shortconv_encoder"


def rms_norm(x, g):
    xf = x.astype(jnp.float32)
    y = xf * lax.rsqrt(jnp.mean(xf * xf, axis=-1, keepdims=True) + EPS)
    return (y * g.astype(jnp.float32)).astype(x.dtype)


def dilated_offsets(window, dilation):
    half = window // (2 * dilation)
    return jnp.arange(-half, half + 1, dtype=jnp.int32) * dilation


def t5_bucket(rel):
    nb = NUM_BUCKETS // 2
    max_exact = nb // 2
    ret = jnp.where(rel > 0, nb, 0)
    n = jnp.abs(rel)
    nf = jnp.maximum(n, 1).astype(jnp.float32)
    large = max_exact + (jnp.log(nf / max_exact) / math.log(MAX_DISTANCE / max_exact)
                         * (nb - max_exact)).astype(jnp.int32)
    large = jnp.minimum(large, nb - 1)
    return ret + jnp.where(n < max_exact, n, large)


def dilated_attention(q, k, v, rel_bias):
    B, S, H, hd = q.shape
    qf = (q.astype(jnp.float32) * (hd ** -0.5)).transpose(0, 2, 1, 3)
    kf = k.astype(jnp.float32).transpose(0, 2, 1, 3)
    vf = v.astype(jnp.float32).transpose(0, 2, 1, 3)
    offsets = [dilated_offsets(w, d) for (w, d) in DILATED_CONFIGS]
    biases = [rel_bias.astype(jnp.float32)[t5_bucket(o)].T for o in offsets]

    def block(n):
        start = n * Q_BLOCK
        qb = lax.dynamic_slice_in_dim(qf, start, Q_BLOCK, axis=2)
        pos = start + jnp.arange(Q_BLOCK, dtype=jnp.int32)
        lses, outs = [], []
        for off, bias in zip(offsets, biases):
            idx = pos[:, None] + off[None, :]
            valid = (idx >= 0) & (idx < S)
            idx = jnp.clip(idx, 0, S - 1)
            kg = jnp.take(kf, idx, axis=2)
            vg = jnp.take(vf, idx, axis=2)
            s = jnp.einsum('bhqd,bhqmd->bhqm', qb, kg) + bias[None, :, None, :]
            s = jnp.where(valid[None, None], s, -jnp.inf)
            lse = jax.nn.logsumexp(s, axis=-1)
            p = jnp.exp(s - lse[..., None])
            outs.append(jnp.einsum('bhqm,bhqmd->bhqd', p, vg))
            lses.append(lse)
        w = jax.nn.softmax(jnp.stack(lses, axis=0), axis=0)
        return jnp.sum(w[..., None] * jnp.stack(outs, axis=0), axis=0)

    o = lax.map(block, jnp.arange(S // Q_BLOCK, dtype=jnp.int32))
    return o.transpose(1, 0, 3, 2, 4).reshape(B, S, H * hd)


def short_conv(z, w):
    zp = jnp.pad(z, ((0, 0), (1, 1), (0, 0)))
    return zp[:, :-2] * w[0] + zp[:, 1:-1] * w[1] + zp[:, 2:] * w[2]


def setup_inputs(seed: int = 0) -> dict:
    key = jax.random.key(seed)
    ks = jax.random.split(key, 10)
    f32 = jnp.float32
    x = jax.random.normal(ks[0], (BATCH, SEQ, D_MODEL), f32)
    c = jax.random.normal(ks[1], (BATCH, D_MODEL), f32)
    w_ada = jax.random.normal(ks[2], (DEPTH, D_MODEL, 3 * D_MODEL), f32) * (0.5 * D_MODEL ** -0.5)
    b_ada = jax.random.normal(ks[3], (DEPTH, 3 * D_MODEL), f32) * 0.02
    pre_norm_g = 1.0 + 0.05 * jax.random.normal(ks[4], (DEPTH, D_MODEL), f32)
    w_in = jax.random.normal(ks[5], (DEPTH, D_MODEL, IN_COLS), f32) * (D_MODEL ** -0.5)
    conv_w = jax.random.normal(ks[6], (DEPTH, CONV_K, CONV_WIDTH), f32) * (CONV_K ** -0.5)
    rel_bias = jax.random.normal(ks[7], (NUM_BUCKETS, N_ATT_HEADS), f32) * 0.5
    w_out = jax.random.normal(ks[8], (DEPTH, MIX_WIDTH, D_MODEL), f32) * (MIX_WIDTH ** -0.5)
    post_norm_g = 1.0 + 0.05 * jax.random.normal(ks[9], (DEPTH, D_MODEL), f32)
    return {"x": x, "c": c, "w_ada": w_ada, "b_ada": b_ada, "pre_norm_g": pre_norm_g,
            "w_in": w_in, "conv_w": conv_w, "rel_bias": rel_bias, "w_out": w_out,
            "post_norm_g": post_norm_g}


def reference(x, c, w_ada, b_ada, pre_norm_g, w_in, conv_w, rel_bias, w_out, post_norm_g):
    B, S, _ = x.shape
    A, Cw = ATT_WIDTH, CONV_WIDTH
    c_act = jax.nn.silu(c)
    for l in range(DEPTH):
        mod = c_act @ w_ada[l] + b_ada[l]
        shift, scale, gate = jnp.split(mod, 3, axis=-1)
        h = rms_norm(x, pre_norm_g[l]) * (1.0 + scale[:, None, :]) + shift[:, None, :]
        proj = h @ w_in[l]
        q = proj[..., 0:A].reshape(B, S, N_ATT_HEADS, HEAD_DIM)
        k = proj[..., A:2 * A].reshape(B, S, N_ATT_HEADS, HEAD_DIM)
        v = proj[..., 2 * A:3 * A].reshape(B, S, N_ATT_HEADS, HEAD_DIM)
        g_att = proj[..., 3 * A:4 * A]
        o0 = 4 * A
        u = proj[..., o0:o0 + Cw]
        b_gate = proj[..., o0 + Cw:o0 + 2 * Cw]
        c_gate = proj[..., o0 + 2 * Cw:o0 + 3 * Cw]
        g_conv = proj[..., o0 + 3 * Cw:o0 + 4 * Cw]
        att = dilated_attention(q, k, v, rel_bias).astype(h.dtype) * jax.nn.silu(g_att)
        cnv = b_gate * short_conv(c_gate * u, conv_w[l]) * jax.nn.silu(g_conv)
        mix = jnp.concatenate([att, cnv], axis=-1) @ w_out[l]
        x = x + gate[:, None, :] * rms_norm(mix, post_norm_g[l])
    return x
```

```python
import functools
import math

import numpy as np
import jax
import jax.numpy as jnp
from jax import lax
from jax.experimental import pallas as pl
from jax.experimental.pallas import tpu as pltpu

HEAD_DIM = 64
N_HEADS = 8
ATT_WIDTH = N_HEADS * HEAD_DIM
CONV_WIDTH = 512
DILATED_CONFIGS = ((128, 1), (512, 4), (2048, 16))
NUM_BUCKETS = 32
MAX_DISTANCE = 1024
EPS = 1e-6

HALF = 64
N_OFFSETS = 2 * HALF + 1
Q_BLOCK = 128
K_BLOCK = 256
LANES = 128
MASKED = -1e30

VMEM_LIMIT = 56 * 1024 * 1024


def _silu(x):
    return x * (1.0 / (1.0 + jnp.exp(-x)))


def _adaln_kernel(c_ref, w_ref, b_ref, o_ref):
    c_act = _silu(c_ref[...])
    o_ref[0] = jnp.dot(c_act, w_ref[0], preferred_element_type=jnp.float32) + b_ref[0]


def _adaln(c, w_ada, b_ada):
    depth, d_model, n_out = w_ada.shape
    batch = c.shape[0]
    tn = 768
    return pl.pallas_call(
        _adaln_kernel,
        grid=(depth, n_out // tn),
        in_specs=[
            pl.BlockSpec((batch, d_model), lambda l, j: (0, 0)),
            pl.BlockSpec((1, d_model, tn), lambda l, j: (l, 0, j)),
            pl.BlockSpec((1, 1, tn), lambda l, j: (l, 0, j)),
        ],
        out_specs=pl.BlockSpec((1, batch, tn), lambda l, j: (l, 0, j)),
        out_shape=jax.ShapeDtypeStruct((depth, batch, n_out), jnp.float32),
        compiler_params=pltpu.CompilerParams(
            dimension_semantics=("arbitrary", "arbitrary"),
            vmem_limit_bytes=VMEM_LIMIT),
        name="adaln",
    )(c, w_ada, b_ada.reshape(depth, 1, n_out))


def _inproj_kernel(x_ref, mod_ref, g_ref, w_ref,
                   q_ref, k_ref, v_ref, ga_ref, cu_ref, bg_ref):
    x = x_ref[0]
    shift = mod_ref[0, 0:1, :]
    scale = mod_ref[0, 1:2, :]
    ms = jnp.mean(x * x, axis=-1, keepdims=True)
    y = x * lax.rsqrt(ms + EPS) * g_ref[...]
    h = (y * (1.0 + scale) + shift).astype(jnp.bfloat16)

    def proj(j):
        return jnp.dot(h, w_ref[:, j * 512:(j + 1) * 512],
                       preferred_element_type=jnp.float32)

    q_ref[0] = (proj(0) * (HEAD_DIM ** -0.5)).astype(q_ref.dtype)
    k_ref[0] = proj(1).astype(k_ref.dtype)
    v_ref[0] = proj(2).astype(v_ref.dtype)
    ga_ref[0] = _silu(proj(3)).astype(ga_ref.dtype)
    u = proj(4)
    b_gate = proj(5)
    cu_ref[0] = (proj(6) * u).astype(cu_ref.dtype)
    bg_ref[0] = (b_gate * _silu(proj(7))).astype(bg_ref.dtype)


def _inproj(x, mod_l, pre_g, w_in_bf16, tm):
    batch, seq, d_model = x.shape
    n_cols = w_in_bf16.shape[1]
    row_spec = lambda width: pl.BlockSpec((1, tm, width), lambda b, i: (b, i, 0))
    act = lambda dt: jax.ShapeDtypeStruct((batch, seq, 512), dt)
    return pl.pallas_call(
        _inproj_kernel,
        grid=(batch, seq // tm),
        in_specs=[
            row_spec(d_model),
            pl.BlockSpec((1, 3, d_model), lambda b, i: (b, 0, 0)),
            pl.BlockSpec((1, d_model), lambda b, i: (0, 0)),
            pl.BlockSpec((d_model, n_cols), lambda b, i: (0, 0)),
        ],
        out_specs=[row_spec(512)] * 6,
        out_shape=[act(jnp.bfloat16)] * 3 + [act(jnp.float32)] * 3,
        compiler_params=pltpu.CompilerParams(
            dimension_semantics=("arbitrary", "arbitrary"),
            vmem_limit_bytes=VMEM_LIMIT),
        name="inproj",
    )(x, mod_l, pre_g, w_in_bf16)


def _t5_bucket_np(rel):
    nb = NUM_BUCKETS // 2
    max_exact = nb // 2
    rel = np.asarray(rel, np.int64)
    ret = np.where(rel > 0, nb, 0)
    n = np.abs(rel)
    nf = np.maximum(n, 1).astype(np.float32)
    ratio = np.log(nf / np.float32(max_exact)) / np.float32(math.log(MAX_DISTANCE / max_exact))
    large = max_exact + (ratio * np.float32(nb - max_exact)).astype(np.int32)
    large = np.minimum(large, nb - 1)
    return ret + np.where(n < max_exact, n, large)


def _bucket_onehot_t(dilation):
    offs = np.arange(-HALF, HALF + 1) * dilation
    bucket = _t5_bucket_np(offs)
    oh = np.zeros((NUM_BUCKETS, K_BLOCK), np.float32)
    oh[bucket, np.arange(N_OFFSETS)] = 1.0
    return oh


TILE_SHIFTS = (0, HALF, 2 * HALF)


def _build_bias_tiles(rel_t_ref, onehot_ref, bias_scr):
    rel_t = rel_t_ref[...]
    onehot = onehot_ref[...]
    bias_vec = jnp.zeros((N_HEADS, K_BLOCK), jnp.float32)
    for b in range(NUM_BUCKETS):
        bias_vec = bias_vec + rel_t[:, b:b + 1] * onehot[b:b + 1, :]
    ii = lax.broadcasted_iota(jnp.int32, (Q_BLOCK, K_BLOCK), 0)
    jj = lax.broadcasted_iota(jnp.int32, (Q_BLOCK, K_BLOCK), 1)
    for s, shift in enumerate(TILE_SHIFTS):
        g = pltpu.roll(bias_vec, (shift - HALF) % K_BLOCK, 1)
        rel = jj - ii - shift
        in_band = (rel >= -HALF) & (rel <= HALF)
        for h in range(N_HEADS):
            row = jnp.broadcast_to(g[h:h + 1, :], (Q_BLOCK, K_BLOCK))
            t = pltpu.roll(row, 0, 1, stride=1, stride_axis=0)
            bias_scr[s, h] = jnp.where(in_band, t, MASKED)


def _attn_kernel(rel_t_ref, onehot_ref, q_ref, k_ref, v_ref, o_ref, lse_ref,
                 bias_scr, *, class_len, tq):
    first = ((pl.program_id(0) == 0) & (pl.program_id(1) == 0)
             & (pl.program_id(2) == 0))

    @pl.when(first)
    def _():
        _build_bias_tiles(rel_t_ref, onehot_ref, bias_scr)

    chunk = pl.program_id(2)
    low_q = lax.broadcasted_iota(jnp.int32, (Q_BLOCK, LANES), 1) < HEAD_DIM
    low_k = lax.broadcasted_iota(jnp.int32, (K_BLOCK, LANES), 1) < HEAD_DIM

    def sub_block(j, carry):
        r0 = pl.multiple_of(j * Q_BLOCK, Q_BLOCK)
        i0 = chunk * tq + r0
        kstart = pl.multiple_of(jnp.clip(i0 - HALF, 0, class_len - K_BLOCK), HALF)
        sidx = jnp.where(i0 == 0, 0, jnp.where(i0 == class_len - Q_BLOCK, 2, 1))
        for hp in range(N_HEADS // 2):
            cols = slice(hp * LANES, (hp + 1) * LANES)
            q2 = q_ref[0, pl.ds(r0, Q_BLOCK), cols]
            kw = k_ref[0, pl.ds(kstart, K_BLOCK), cols]
            vw = v_ref[0, pl.ds(kstart, K_BLOCK), cols]
            accs, ms = [], []
            for e in range(2):
                own_q = low_q if e == 0 else ~low_q
                own_k = low_k if e == 0 else ~low_k
                qm = jnp.where(own_q, q2, jnp.zeros_like(q2))
                s = lax.dot_general(qm, kw, (((1,), (1,)), ((), ())),
                                    preferred_element_type=jnp.float32)
                s = s + bias_scr[sidx, 2 * hp + e]
                m = jnp.max(s, axis=1, keepdims=True)
                p = jnp.exp(s - m).astype(jnp.bfloat16)
                va = jnp.where(own_k, vw, jnp.ones_like(vw))
                accs.append(jnp.dot(p, va, preferred_element_type=jnp.float32))
                ms.append(m)
            num = jnp.where(low_q, accs[0], accs[1])
            den = pltpu.roll(jnp.where(low_q, accs[1], accs[0]), HEAD_DIM, 1)
            m2 = jnp.where(low_q, ms[0], ms[1])
            o_ref[0, pl.ds(r0, Q_BLOCK), cols] = (num / den).astype(o_ref.dtype)
            lse_ref[0, pl.ds(r0, Q_BLOCK), cols] = m2 + jnp.log(den)
        return carry

    lax.fori_loop(0, tq // Q_BLOCK, sub_block, 0)


def _banded_attention(q, k, v, rel_t, dilation, out_dtype):
    batch, seq, width = q.shape
    class_len = seq // dilation
    tq = min(512, class_len)
    view = lambda a: a.reshape(batch, class_len, dilation * width)
    onehot = jnp.asarray(_bucket_onehot_t(dilation))
    q_spec = pl.BlockSpec((1, tq, width), lambda b, r, c: (b, c, r))
    kv_spec = pl.BlockSpec((1, class_len, width), lambda b, r, c: (b, 0, r))
    o, lse = pl.pallas_call(
        functools.partial(_attn_kernel, class_len=class_len, tq=tq),
        grid=(batch, dilation, class_len // tq),
        in_specs=[
            pl.BlockSpec((N_HEADS, NUM_BUCKETS), lambda b, r, c: (0, 0)),
            pl.BlockSpec((NUM_BUCKETS, K_BLOCK), lambda b, r, c: (0, 0)),
            q_spec, kv_spec, kv_spec,
        ],
        out_specs=[q_spec, q_spec],
        out_shape=[
            jax.ShapeDtypeStruct((batch, class_len, dilation * width), out_dtype),
            jax.ShapeDtypeStruct((batch, class_len, dilation * width), jnp.float32),
        ],
        scratch_shapes=[pltpu.VMEM((len(TILE_SHIFTS), N_HEADS, Q_BLOCK, K_BLOCK),
                                   jnp.float32)],
        compiler_params=pltpu.CompilerParams(
            dimension_semantics=("arbitrary", "arbitrary", "arbitrary"),
            vmem_limit_bytes=VMEM_LIMIT),
        name=f"attn_d{dilation}",
    )(rel_t, onehot, view(q), view(k), view(v))
    return o.reshape(batch, seq, width), lse.reshape(batch, seq, width)


def _outproj_kernel(o1_ref, o2_ref, o3_ref, l1_ref, l2_ref, l3_ref,
                    ga_ref, cu_ref, cup_ref, cun_ref, bg_ref, x_ref,
                    mod_ref, g_ref, cw_ref, w_ref, out_ref, *, tm):
    i = pl.program_id(1)
    n_i = pl.num_programs(1)

    l1, l2, l3 = l1_ref[0], l2_ref[0], l3_ref[0]
    mx = jnp.maximum(jnp.maximum(l1, l2), l3)
    e1, e2, e3 = jnp.exp(l1 - mx), jnp.exp(l2 - mx), jnp.exp(l3 - mx)
    merged = (e1 * o1_ref[0] + e2 * o2_ref[0] + e3 * o3_ref[0]) / (e1 + e2 + e3)
    att = (merged * ga_ref[0]).astype(jnp.bfloat16)

    cu = cu_ref[0].astype(jnp.float32)
    prev_row = jnp.where(i > 0, cup_ref[0, 7:8, :].astype(jnp.float32), 0.0)
    next_row = jnp.where(i < n_i - 1, cun_ref[0, 0:1, :].astype(jnp.float32), 0.0)
    rows = lax.broadcasted_iota(jnp.int32, cu.shape, 0)
    cu_m1 = jnp.where(rows == 0, prev_row, pltpu.roll(cu, 1, 0))
    cu_p1 = jnp.where(rows == tm - 1, next_row, pltpu.roll(cu, tm - 1, 0))
    conv = cu_m1 * cw_ref[0:1, :] + cu * cw_ref[1:2, :] + cu_p1 * cw_ref[2:3, :]
    cnv = (bg_ref[0] * conv).astype(jnp.bfloat16)

    mix = (jnp.dot(att, w_ref[0:ATT_WIDTH, :], preferred_element_type=jnp.float32)
           + jnp.dot(cnv, w_ref[ATT_WIDTH:, :], preferred_element_type=jnp.float32))
    ms = jnp.mean(mix * mix, axis=-1, keepdims=True)
    y = mix * lax.rsqrt(ms + EPS) * g_ref[...]
    gate = mod_ref[0, 2:3, :]
    out_ref[0] = x_ref[0] + gate * y


def _outproj(outs, lses, ga, cu, bg, x, mod_l, post_g, conv_w, w_out_bf16, tm):
    batch, seq, d_model = x.shape
    n_blk8 = seq // 8
    row_spec = lambda width: pl.BlockSpec((1, tm, width), lambda b, i: (b, i, 0))
    prev_spec = pl.BlockSpec(
        (1, 8, CONV_WIDTH), lambda b, i: (b, jnp.maximum(i * (tm // 8) - 1, 0), 0))
    next_spec = pl.BlockSpec(
        (1, 8, CONV_WIDTH), lambda b, i: (b, jnp.minimum((i + 1) * (tm // 8), n_blk8 - 1), 0))
    return pl.pallas_call(
        functools.partial(_outproj_kernel, tm=tm),
        grid=(batch, seq // tm),
        in_specs=[row_spec(512)] * 8 + [
            prev_spec, next_spec, row_spec(512), row_spec(d_model),
            pl.BlockSpec((1, 3, d_model), lambda b, i: (b, 0, 0)),
            pl.BlockSpec((1, d_model), lambda b, i: (0, 0)),
            pl.BlockSpec((3, CONV_WIDTH), lambda b, i: (0, 0)),
            pl.BlockSpec((ATT_WIDTH + CONV_WIDTH, d_model), lambda b, i: (0, 0)),
        ],
        out_specs=row_spec(d_model),
        out_shape=jax.ShapeDtypeStruct((batch, seq, d_model), jnp.float32),
        compiler_params=pltpu.CompilerParams(
            dimension_semantics=("arbitrary", "arbitrary"),
            vmem_limit_bytes=VMEM_LIMIT),
        name="outproj",
    )(*outs, *lses, ga, cu, cu, cu, bg, x, mod_l, post_g, conv_w, w_out_bf16)


def kernel(x, c, w_ada, b_ada, pre_norm_g, w_in, conv_w, rel_bias, w_out, post_norm_g):
    batch, seq, d_model = x.shape
    depth = w_ada.shape[0]
    tm = 512

    mod = _adaln(c, w_ada, b_ada).reshape(depth, batch, 3, d_model)
    rel_t = rel_bias.T
    w_in_b = w_in.astype(jnp.bfloat16)
    w_out_b = w_out.astype(jnp.bfloat16)

    for l in range(depth):
        q, k, v, ga, cu, bg = _inproj(x, mod[l], pre_norm_g[l][None], w_in_b[l], tm)
        outs, lses = [], []
        for _, dilation in DILATED_CONFIGS:
            o, lse = _banded_attention(q, k, v, rel_t, dilation, jnp.float32)
            outs.append(o)
            lses.append(lse)
        x = _outproj(outs, lses, ga, cu, bg, x, mod[l], post_norm_g[l][None],
                     conv_w[l], w_out_b[l], tm)
    return x
```
